```python
import math, functools
import jax, jax.numpy as jnp
from jax import lax
import numpy as np

D_MODEL = 2048
BATCH = 4
SEQ = 2048
DEPTH = 2
DEC_BATCH = 8
DEC_SEQ = 1
PAST_LEN = 16384
PAGE_SIZE = 128

MIX_WIDTH = D_MODEL
ATTN_WIDTH = MIX_WIDTH // 2
GMLP_WIDTH = MIX_WIDTH - ATTN_WIDTH
N_HEADS = 8
HEAD_DIM = ATTN_WIDTH // (2 * N_HEADS)
QK_DIM = 2 * HEAD_DIM
V_DIM = ATTN_WIDTH // N_HEADS
N_GROUPS = 8
GROUP_DIM = GMLP_WIDTH // N_GROUPS
CHUNK = 128
D_FF = 4 * D_MODEL
PLE_DIM = 256
Q_BLOCK = 128
IN_WIDTH = 3 * ATTN_WIDTH + 2 * GMLP_WIDTH
RMS_EPS = 1e-6
NEG_INF = -1e30

kernel_name = "hymba_diffattn_gmlp_decode_step"


def rmsnorm(x, g):
    xf = x.astype(jnp.float32)
    y = xf * lax.rsqrt(jnp.mean(xf * xf, axis=-1, keepdims=True) + RMS_EPS)
    return (y * g.astype(jnp.float32)).astype(x.dtype)


def alibi_slopes():
    return jnp.asarray([2.0 ** (-8.0 * (h + 1) / N_HEADS) for h in range(N_HEADS)], jnp.float32)


def diff_attend(q, k, v, qpos, kpos, lam):
    qf = q.astype(jnp.float32) * (HEAD_DIM ** -0.5)
    kf = k.astype(jnp.float32)
    dist = (qpos[:, None] - kpos[None, :]).astype(jnp.float32)
    bias = -alibi_slopes()[:, None, None] * jnp.abs(dist)[None]
    bias = jnp.where((kpos[None, :] <= qpos[:, None])[None], bias, NEG_INF)
    s1 = jnp.einsum('bqhd,bkhd->bhqk', qf[..., :HEAD_DIM], kf[..., :HEAD_DIM]) + bias
    s2 = jnp.einsum('bqhd,bkhd->bhqk', qf[..., HEAD_DIM:], kf[..., HEAD_DIM:]) + bias
    w = jax.nn.softmax(s1, axis=-1) - lam * jax.nn.softmax(s2, axis=-1)
    return jnp.einsum('bhqk,bkhd->bqhd', w, v.astype(jnp.float32))


def prompt_attention(q, k, v, lam):
    b, s = q.shape[0], q.shape[1]
    nb = s // Q_BLOCK
    qb = q.reshape(b, nb, Q_BLOCK, N_HEADS, QK_DIM).transpose(1, 0, 2, 3, 4)
    kpos = jnp.arange(s, dtype=jnp.int32)

    def block(args):
        q_blk, start = args
        qpos = start + jnp.arange(Q_BLOCK, dtype=jnp.int32)
        return diff_attend(q_blk, k, v, qpos, kpos, lam)

    out = lax.map(block, (qb, jnp.arange(nb, dtype=jnp.int32) * Q_BLOCK))
    return out.transpose(1, 0, 2, 3, 4).reshape(b, s, N_HEADS, V_DIM)


def sample_attention(q, k, v, lam, k_pool, v_pool, page_table):
    db, n_pages = page_table.shape
    past = n_pages * PAGE_SIZE
    k_past = k_pool[page_table].reshape(db, past, N_HEADS, QK_DIM)
    v_past = v_pool[page_table].reshape(db, past, N_HEADS, V_DIM)
    k_all = jnp.concatenate([k_past, k.astype(k_past.dtype)], axis=1)
    v_all = jnp.concatenate([v_past, v.astype(v_past.dtype)], axis=1)
    t = q.shape[1]
    kpos = jnp.arange(past + t, dtype=jnp.int32)
    qpos = past + jnp.arange(t, dtype=jnp.int32)
    return diff_attend(q, k_all, v_all, qpos, kpos, lam)


def chunk_gate(u, gv, w_s, b_s):
    b, t, _ = gv.shape
    n_chunks = -(-t // CHUNK)
    pad = n_chunks * CHUNK - t
    gvp = jnp.pad(gv, ((0, 0), (0, pad), (0, 0))).reshape(b, n_chunks, CHUNK, N_GROUPS, GROUP_DIM)
    w = jnp.where(jnp.tril(jnp.ones((CHUNK, CHUNK), dtype=bool)), w_s, 0)
    mixed = jnp.einsum('gts,bnsgc->bntgc', w, gvp) + b_s.T[None, None, :, :, None]
    mixed = mixed.reshape(b, n_chunks * CHUNK, GMLP_WIDTH)[:, :t]
    return u * mixed.astype(u.dtype)


def trunk_layer(h, p_i, lam_init, attend, g_mix, w_in, lam_q1, lam_k1, lam_q2, lam_k2,
                g_sub, g_v, w_s, b_s, w_out, g_ffn, w_up, w_down, g_ple, w_ple_gate, w_ple_proj):
    b, t, _ = h.shape
    z = rmsnorm(h, g_mix) @ w_in
    z_q, z_k, z_v, z_u, z_g = jnp.split(
        z, [ATTN_WIDTH, 2 * ATTN_WIDTH, 3 * ATTN_WIDTH, 3 * ATTN_WIDTH + GMLP_WIDTH], axis=-1)
    q = z_q.reshape(b, t, N_HEADS, QK_DIM)
    k = z_k.reshape(b, t, N_HEADS, QK_DIM)
    v = z_v.reshape(b, t, N_HEADS, V_DIM)
    lam = (jnp.exp(jnp.sum(lam_q1.astype(jnp.float32) * lam_k1.astype(jnp.float32)))
           - jnp.exp(jnp.sum(lam_q2.astype(jnp.float32) * lam_k2.astype(jnp.float32)))
           + lam_init)
    o = rmsnorm(attend(q, k, v, lam), g_sub) * (1.0 - lam_init)
    attn_out = o.reshape(b, t, ATTN_WIDTH).astype(h.dtype)
    u = jax.nn.gelu(z_u)
    gv = rmsnorm(jax.nn.gelu(z_g), g_v)
    sg = chunk_gate(u, gv, w_s, b_s)
    h = h + jnp.concatenate([attn_out, sg], axis=-1) @ w_out
    h = h + jnp.square(jax.nn.relu(rmsnorm(h, g_ffn) @ w_up)) @ w_down
    gate = jax.nn.sigmoid(rmsnorm(h, g_ple) @ w_ple_gate)
    h = h + (p_i.astype(h.dtype) @ w_ple_proj) * gate
    return h, k, v, gv


def setup_inputs(seed: int = 0) -> dict:
    key = jax.random.key(seed)
    ks = jax.random.split(key, 32)
    n_pages = PAST_LEN // PAGE_SIZE
    n_used = DEC_BATCH * n_pages
    n_pool = n_used + max(1, n_used // 4)
    perm = jax.random.permutation(ks[0], n_pool)
    page_table = perm[:n_used].reshape(DEC_BATCH, n_pages).astype(jnp.int32)

    def nrm(k, shape, s):
        return jax.random.normal(k, shape, jnp.float32) * s

    def gain(k, shape):
        return 1.0 + 0.05 * jax.random.normal(k, shape, jnp.float32)

    return {
        "x_prompt": nrm(ks[1], (BATCH, SEQ, D_MODEL), 1.0),
        "x_sample": nrm(ks[2], (DEC_BATCH, DEC_SEQ, D_MODEL), 1.0),
        "cache_k": nrm(ks[3], (DEPTH, n_pool, PAGE_SIZE, N_HEADS, QK_DIM), 1.0),
        "cache_v": nrm(ks[4], (DEPTH, n_pool, PAGE_SIZE, N_HEADS, V_DIM), 1.0),
        "page_table": page_table,
        "p_prompt": nrm(ks[5], (DEPTH, BATCH, SEQ, PLE_DIM), 1.0),
        "p_sample": nrm(ks[6], (DEPTH, DEC_BATCH, DEC_SEQ, PLE_DIM), 1.0),
        "g_mix": gain(ks[7], (DEPTH, D_MODEL)),
        "w_in": nrm(ks[8], (DEPTH, D_MODEL, IN_WIDTH), D_MODEL ** -0.5),
        "lam_q1": nrm(ks[9], (DEPTH, HEAD_DIM), 0.1),
        "lam_k1": nrm(ks[10], (DEPTH, HEAD_DIM), 0.1),
        "lam_q2": nrm(ks[11], (DEPTH, HEAD_DIM), 0.1),
        "lam_k2": nrm(ks[12], (DEPTH, HEAD_DIM), 0.1),
        "g_sub": gain(ks[13], (DEPTH, V_DIM)),
        "g_v": gain(ks[14], (DEPTH, GMLP_WIDTH)),
        "w_s": nrm(ks[15], (DEPTH, N_GROUPS, CHUNK, CHUNK), CHUNK ** -0.5),
        "b_s": 1.0 + nrm(ks[16], (DEPTH, N_GROUPS, CHUNK), 0.1),
        "w_out": nrm(ks[17], (DEPTH, MIX_WIDTH, D_MODEL), MIX_WIDTH ** -0.5),
        "g_ffn": gain(ks[18], (DEPTH, D_MODEL)),
        "w_up": nrm(ks[19], (DEPTH, D_MODEL, D_FF), D_MODEL ** -0.5),
        "w_down": nrm(ks[20], (DEPTH, D_FF, D_MODEL), (1.5 * D_FF) ** -0.5),
        "g_ple": gain(ks[21], (DEPTH, D_MODEL)),
        "w_ple_gate": nrm(ks[22], (DEPTH, D_MODEL, D_MODEL), D_MODEL ** -0.5),
        "w_ple_proj": nrm(ks[23], (DEPTH, PLE_DIM, D_MODEL), PLE_DIM ** -0.5),
        "g_final": gain(ks[24], (D_MODEL,)),
    }


def reference(x_prompt, x_sample, cache_k, cache_v, page_table, p_prompt, p_sample,
              g_mix, w_in, lam_q1, lam_k1, lam_q2, lam_k2, g_sub, g_v, w_s, b_s, w_out,
              g_ffn, w_up, w_down, g_ple, w_ple_gate, w_ple_proj, g_final):
    hp, hs = x_prompt, x_sample
    kp_list, vp_list, ks_list, vs_list, gs_list = [], [], [], [], []
    for i in range(DEPTH):
        lam_init = 0.8 - 0.6 * math.exp(-0.3 * i)
        w = (g_mix[i], w_in[i], lam_q1[i], lam_k1[i], lam_q2[i], lam_k2[i], g_sub[i], g_v[i],
             w_s[i], b_s[i], w_out[i], g_ffn[i], w_up[i], w_down[i], g_ple[i],
             w_ple_gate[i], w_ple_proj[i])
        hp, k_p, v_p, _ = trunk_layer(hp, p_prompt[i], lam_init, prompt_attention, *w)
        kp_list.append(k_p)
        vp_list.append(v_p)
        attend_s = functools.partial(sample_attention, k_pool=cache_k[i], v_pool=cache_v[i],
                                     page_table=page_table)
        hs, k_s, v_s, gv_s = trunk_layer(hs, p_sample[i], lam_init, attend_s, *w)
        ks_list.append(k_s)
        vs_list.append(v_s)
        gs_list.append(gv_s)
    y_prompt = rmsnorm(hp, g_final)
    y_sample = rmsnorm(hs, g_final)
    return (y_prompt, y_sample, jnp.stack(kp_list), jnp.stack(vp_list),
            jnp.stack(ks_list), jnp.stack(vs_list), jnp.stack(gs_list))
```

```python
import functools
import math

import jax
import jax.numpy as jnp
from jax import lax
from jax.experimental import pallas as pl
from jax.experimental.pallas import tpu as pltpu

F32 = jnp.float32
BF16 = jnp.bfloat16

D_MODEL = 2048
N_HEADS = 8
HEAD_DIM = 64
QK_DIM = 128
V_DIM = 128
ATTN_WIDTH = 1024
GMLP_WIDTH = 1024
N_GROUPS = 8
GROUP_DIM = 128
CHUNK = 128
D_FF = 4 * D_MODEL
PLE_DIM = 256
PAGE_SIZE = 128
RMS_EPS = 1e-6
NEG_INF = -1e30

VMEM_LIMIT_BYTES = 56 * 1024 * 1024

TM_PROMPT = 512
TF = 512
TQ = 256
PAGES_PER_STEP = 4


def _params(n_axes):
    return pltpu.CompilerParams(
        dimension_semantics=("arbitrary",) * n_axes,
        vmem_limit_bytes=VMEM_LIMIT_BYTES)


def _rms(x, g):
    ms = jnp.mean(x * x, axis=-1, keepdims=True)
    return x * lax.rsqrt(ms + RMS_EPS) * g


def _dot_nt(a, b):
    return lax.dot_general(a, b, (((1,), (1,)), ((), ())), preferred_element_type=F32)


def _norm_kernel(x_ref, g_ref, o_ref):
    o_ref[...] = _rms(x_ref[...], g_ref[...]).astype(o_ref.dtype)


def _norm_call(x, g, tm):
    m, d = x.shape
    return pl.pallas_call(
        _norm_kernel,
        grid=(m // tm,),
        in_specs=[pl.BlockSpec((tm, d), lambda i: (i, 0)),
                  pl.BlockSpec((1, d), lambda i: (0, 0))],
        out_specs=pl.BlockSpec((tm, d), lambda i: (i, 0)),
        out_shape=jax.ShapeDtypeStruct((m, d), BF16),
        compiler_params=_params(1),
        name="rmsnorm",
    )(x, g)


def _store_heads(ref, z):
    tm = z.shape[0]
    for h in range(N_HEADS):
        ref[pl.ds(h, tm, stride=N_HEADS), :] = z[:, h * QK_DIM:(h + 1) * QK_DIM]


def _inproj_kernel(xn_ref, w_ref, gv_g_ref, qa_ref, qb_ref, k_ref, v_ref, u_ref, gv_ref):
    grp = pl.program_id(1)
    z = jnp.dot(xn_ref[...], w_ref[...], preferred_element_type=F32)

    @pl.when(grp == 0)
    def _():
        zq = z * (HEAD_DIM ** -0.5)
        lane = lax.broadcasted_iota(jnp.int32, zq.shape, 1)
        first = (lane & (QK_DIM - 1)) < HEAD_DIM
        qa_ref[...] = jnp.where(first, zq, 0.0).astype(qa_ref.dtype)
        qb_ref[...] = jnp.where(first, 0.0, zq).astype(qb_ref.dtype)

    @pl.when(grp == 1)
    def _():
        _store_heads(k_ref, z)

    @pl.when(grp == 2)
    def _():
        _store_heads(v_ref, z)

    @pl.when(grp == 3)
    def _():
        u_ref[...] = jax.nn.gelu(z).astype(u_ref.dtype)

    @pl.when(grp == 4)
    def _():
        gv_ref[...] = _rms(jax.nn.gelu(z), gv_g_ref[...]).astype(gv_ref.dtype)


def _inproj_call(xn, w_in_b, g_v, layer, tm, act_dtype):
    m = xn.shape[0]
    row = lambda i, g: (i, 0)
    wide = pl.BlockSpec((tm, 1024), row)
    heads = pl.BlockSpec((tm * N_HEADS, QK_DIM), row)
    out_shape = (
        jax.ShapeDtypeStruct((m, ATTN_WIDTH), BF16),
        jax.ShapeDtypeStruct((m, ATTN_WIDTH), BF16),
        jax.ShapeDtypeStruct((m * N_HEADS, QK_DIM), F32),
        jax.ShapeDtypeStruct((m * N_HEADS, V_DIM), F32),
        jax.ShapeDtypeStruct((m, GMLP_WIDTH), act_dtype),
        jax.ShapeDtypeStruct((m, GMLP_WIDTH), act_dtype),
    )
    return pl.pallas_call(
        _inproj_kernel,
        grid=(m // tm, 5),
        in_specs=[pl.BlockSpec((tm, D_MODEL), row),
                  pl.BlockSpec((None, D_MODEL, 1024), lambda i, g: (layer, 0, g)),
                  pl.BlockSpec((1, GMLP_WIDTH), lambda i, g: (0, 0))],
        out_specs=[wide, wide, heads, heads, wide, wide],
        out_shape=out_shape,
        compiler_params=_params(2),
        name="inproj",
    )(xn, w_in_b, g_v)


def _lam(lam_ref, lam_init):
    lv = lam_ref[...]
    a = jnp.sum(lv[0:1] * lv[1:2], axis=-1, keepdims=True)
    b = jnp.sum(lv[2:3] * lv[3:4], axis=-1, keepdims=True)
    return jnp.exp(a) - jnp.exp(b) + lam_init


def _neg_slope(head_plus_one):
    bits = (127 - head_plus_one) << 23
    return -lax.bitcast_convert_type(bits, F32)


def _attn_kernel(lam_ref, qa_ref, qb_ref, k_ref, v_ref, gsub_ref, o_ref, kb_ref, vb_ref,
                 *, lam_init, seq):
    head = pl.program_id(1)
    qi = pl.program_id(2)

    @pl.when(qi == 0)
    def _():
        kb_ref[...] = k_ref[pl.ds(head, seq, stride=N_HEADS), :].astype(BF16)
        vb_ref[...] = v_ref[pl.ds(head, seq, stride=N_HEADS), :].astype(BF16)

    qq = jnp.concatenate([qa_ref[...], qb_ref[...]], axis=0)
    rows = 2 * TQ
    r = lax.broadcasted_iota(jnp.int32, (rows, TQ), 0) & (TQ - 1)
    c = lax.broadcasted_iota(jnp.int32, (rows, TQ), 1)
    rel = (r - c).astype(F32)
    nslope = _neg_slope(jnp.full((1, 1), head + 1, jnp.int32))
    rel_bias = nslope * rel

    def step(j, carry, diagonal):
        m, l, acc = carry
        start = pl.multiple_of(j * TQ, TQ)
        kblk = kb_ref[pl.ds(start, TQ), :]
        vblk = vb_ref[pl.ds(start, TQ), :]
        s = _dot_nt(qq, kblk)
        base = jnp.full((1, 1), (qi - j) * TQ, jnp.int32).astype(F32)
        s = s + (rel_bias + nslope * base)
        if diagonal:
            s = jnp.where(rel >= 0.0, s, NEG_INF)
        m_new = jnp.maximum(m, jnp.max(s, axis=-1, keepdims=True))
        alpha = jnp.exp(m - m_new)
        p = jnp.exp(s - m_new)
        l = alpha * l + jnp.sum(p, axis=-1, keepdims=True)
        acc = alpha * acc + jnp.dot(p.astype(BF16), vblk, preferred_element_type=F32)
        return m_new, l, acc

    init = (jnp.full((rows, 1), NEG_INF, F32), jnp.zeros((rows, 1), F32),
            jnp.zeros((rows, V_DIM), F32))
    carry = lax.fori_loop(0, qi, functools.partial(step, diagonal=False), init)
    m, l, acc = step(qi, carry, True)

    o = acc / l
    lam = _lam(lam_ref, lam_init)
    o = o[:TQ] - lam * o[TQ:]
    o_ref[...] = (_rms(o, gsub_ref[...]) * (1.0 - lam_init)).astype(o_ref.dtype)


def _attn_call(lam_vecs, qa, qb, k, v, g_sub, lam_init, batch, seq):
    nq = seq // TQ
    qspec = pl.BlockSpec((TQ, QK_DIM), lambda b, h, i: (b * nq + i, h))
    kvspec = pl.BlockSpec((seq * N_HEADS, QK_DIM), lambda b, h, i: (b, 0),
                          pipeline_mode=pl.Buffered(1))
    return pl.pallas_call(
        functools.partial(_attn_kernel, lam_init=lam_init, seq=seq),
        grid=(batch, N_HEADS, nq),
        in_specs=[pl.BlockSpec((4, HEAD_DIM), lambda b, h, i: (0, 0)),
                  qspec, qspec, kvspec, kvspec,
                  pl.BlockSpec((1, V_DIM), lambda b, h, i: (0, 0))],
        out_specs=pl.BlockSpec((TQ, V_DIM), lambda b, h, i: (b * nq + i, h)),
        out_shape=jax.ShapeDtypeStruct((batch * seq, ATTN_WIDTH), BF16),
        scratch_shapes=[pltpu.VMEM((seq, QK_DIM), BF16), pltpu.VMEM((seq, V_DIM), BF16)],
        compiler_params=_params(3),
        name="prompt_attn",
    )(lam_vecs, qa, qb, k, v, g_sub)


def _sattn_kernel(pt_ref, lam_ref, qa_ref, qb_ref, kn_ref, vn_ref, gsub_ref, *rest,
                  lam_init, past, n_steps):
    del pt_ref
    npg = PAGES_PER_STEP
    k_refs = rest[:npg]
    v_refs = rest[npg:2 * npg]
    o_ref, m_sc, l_sc, acc_sc = rest[2 * npg:]
    j = pl.program_id(1)
    nrow = 2 * N_HEADS
    page_cols = PAGE_SIZE * N_HEADS
    span = npg * page_cols

    qrows = jnp.concatenate([qa_ref[...], qb_ref[...]], axis=0)

    def own(width):
        row = lax.broadcasted_iota(jnp.int32, (nrow, width), 0)
        col = lax.broadcasted_iota(jnp.int32, (nrow, width), 1)
        return (col & (N_HEADS - 1)) == (row & (N_HEADS - 1))

    @pl.when(j == 0)
    def _():
        kn = kn_ref[...]
        kn_rep = jnp.concatenate([kn] * (QK_DIM // N_HEADS), axis=0).astype(BF16)
        s0 = jnp.where(own(QK_DIM), _dot_nt(qrows, kn_rep), NEG_INF)
        m_sc[...] = jnp.max(s0, axis=-1, keepdims=True)
        l_sc[...] = jnp.ones_like(l_sc)
        vn = vn_ref[...]
        acc_sc[...] = jnp.concatenate([vn, vn], axis=0)

    kcat = jnp.concatenate([r[...].reshape(page_cols, QK_DIM) for r in k_refs],
                           axis=0).astype(BF16)
    vcat = jnp.concatenate([r[...].reshape(page_cols, V_DIM) for r in v_refs],
                           axis=0).astype(BF16)
    s = _dot_nt(qrows, kcat)
    col = lax.broadcasted_iota(jnp.int32, (1, span), 1)
    kpos = j * (npg * PAGE_SIZE) + (col >> 3)
    dist = (past - kpos).astype(F32)
    head1 = (lax.broadcasted_iota(jnp.int32, (nrow, 1), 0) & (N_HEADS - 1)) + 1
    s = jnp.where(own(span), s + _neg_slope(head1) * dist, NEG_INF)
    m = m_sc[...]
    m_new = jnp.maximum(m, jnp.max(s, axis=-1, keepdims=True))
    alpha = jnp.exp(m - m_new)
    p = jnp.exp(s - m_new)
    l_sc[...] = alpha * l_sc[...] + jnp.sum(p, axis=-1, keepdims=True)
    acc_sc[...] = alpha * acc_sc[...] + jnp.dot(p.astype(BF16), vcat,
                                                preferred_element_type=F32)
    m_sc[...] = m_new

    @pl.when(j == n_steps - 1)
    def _():
        o = acc_sc[...] / l_sc[...]
        lam = _lam(lam_ref, lam_init)
        o = o[:N_HEADS] - lam * o[N_HEADS:]
        o_ref[...] = (_rms(o, gsub_ref[...]) * (1.0 - lam_init)).astype(o_ref.dtype)


def _sattn_call(page_table, lam_vecs, qa, qb, k_new, v_new, g_sub, cache_k, cache_v,
                layer, lam_init):
    db, n_pages = page_table.shape
    npg = PAGES_PER_STEP
    n_steps = n_pages // npg
    past = n_pages * PAGE_SIZE
    q3spec = pl.BlockSpec((None, N_HEADS, QK_DIM), lambda b, j, pt: (b, 0, 0))
    newspec = pl.BlockSpec((N_HEADS, QK_DIM), lambda b, j, pt: (b, 0))

    def page_spec(p):
        return pl.BlockSpec((None, None, PAGE_SIZE, N_HEADS, QK_DIM),
                            lambda b, j, pt: (layer, pt[b, j * npg + p], 0, 0, 0))

    grid_spec = pltpu.PrefetchScalarGridSpec(
        num_scalar_prefetch=1,
        grid=(db, n_steps),
        in_specs=[pl.BlockSpec((4, HEAD_DIM), lambda b, j, pt: (0, 0)),
                  q3spec, q3spec, newspec, newspec,
                  pl.BlockSpec((1, V_DIM), lambda b, j, pt: (0, 0))]
                 + [page_spec(p) for p in range(npg)] * 2,
        out_specs=q3spec,
        scratch_shapes=[pltpu.VMEM((2 * N_HEADS, 1), F32), pltpu.VMEM((2 * N_HEADS, 1), F32),
                        pltpu.VMEM((2 * N_HEADS, V_DIM), F32)],
    )
    q3 = lambda a: a.reshape(db, N_HEADS, QK_DIM)
    out = pl.pallas_call(
        functools.partial(_sattn_kernel, lam_init=lam_init, past=past, n_steps=n_steps),
        grid_spec=grid_spec,
        out_shape=jax.ShapeDtypeStruct((db, N_HEADS, V_DIM), F32),
        compiler_params=_params(2),
        name="sample_attn",
    )(page_table, lam_vecs, q3(qa), q3(qb), k_new, v_new, g_sub,
      *([cache_k] * npg), *([cache_v] * npg))
    return out.reshape(db, N_HEADS * V_DIM)


def _mixout_kernel(attn_ref, u_ref, gv_ref, ws_ref, bias_ref, w_ref, h_ref, g_ref,
                   hout_ref, xn_ref, *, single_position):
    u = u_ref[...].astype(F32)
    if single_position:
        mixed = gv_ref[...].astype(F32) * ws_ref[...] + bias_ref[...]
        sg = u * mixed
    else:
        tm = u.shape[0]
        n_chunks = tm // CHUNK
        tr = lax.broadcasted_iota(jnp.int32, (CHUNK, CHUNK), 0)
        tc = lax.broadcasted_iota(jnp.int32, (CHUNK, CHUNK), 1)
        gv = gv_ref[...]
        cols = []
        for g in range(N_GROUPS):
            gs = slice(g * GROUP_DIM, (g + 1) * GROUP_DIM)
            w = jnp.where(tc <= tr, ws_ref[g], 0.0).astype(BF16)
            rhs = jnp.concatenate(
                [gv[n * CHUNK:(n + 1) * CHUNK, gs] for n in range(n_chunks)], axis=1)
            mixed = jnp.dot(w, rhs, preferred_element_type=F32)
            b = bias_ref[:, gs]
            cols.append(jnp.concatenate(
                [mixed[:, n * GROUP_DIM:(n + 1) * GROUP_DIM] + b for n in range(n_chunks)],
                axis=0))
        sg = u * jnp.concatenate(cols, axis=1)
    hw = ATTN_WIDTH
    y = (jnp.dot(attn_ref[...].astype(BF16), w_ref[:hw, :], preferred_element_type=F32)
         + jnp.dot(sg.astype(BF16), w_ref[hw:, :], preferred_element_type=F32))
    h = h_ref[...] + y
    hout_ref[...] = h
    xn_ref[...] = _rms(h, g_ref[...]).astype(xn_ref.dtype)


def _mixout_call(attn, u, gv, ws, bias, w_out_b, h, g_ffn, layer, tm, single_position):
    m = h.shape[0]
    row = lambda i: (i, 0)
    const2 = lambda i: (0, 0)
    if single_position:
        ws_spec = pl.BlockSpec((1, GMLP_WIDTH), const2)
        bias_spec = pl.BlockSpec((1, GMLP_WIDTH), const2)
    else:
        ws_spec = pl.BlockSpec((N_GROUPS, CHUNK, CHUNK), lambda i: (0, 0, 0))
        bias_spec = pl.BlockSpec((CHUNK, GMLP_WIDTH), const2)
    return pl.pallas_call(
        functools.partial(_mixout_kernel, single_position=single_position),
        grid=(m // tm,),
        in_specs=[pl.BlockSpec((tm, ATTN_WIDTH), row),
                  pl.BlockSpec((tm, GMLP_WIDTH), row),
                  pl.BlockSpec((tm, GMLP_WIDTH), row),
                  ws_spec, bias_spec,
                  pl.BlockSpec((None, D_MODEL, D_MODEL), lambda i: (layer, 0, 0)),
                  pl.BlockSpec((tm, D_MODEL), row),
                  pl.BlockSpec((1, D_MODEL), const2)],
        out_specs=[pl.BlockSpec((tm, D_MODEL), row), pl.BlockSpec((tm, D_MODEL), row)],
        out_shape=(jax.ShapeDtypeStruct((m, D_MODEL), F32),
                   jax.ShapeDtypeStruct((m, D_MODEL), BF16)),
        compiler_params=_params(1),
        name="mix_out",
    )(attn, u, gv, ws, bias, w_out_b, h, g_ffn)


def _ffn_kernel(xn_ref, wup_ref, wdn_ref, h_ref, g_ref, hout_ref, xn3_ref, acc_ref, *, n_f):
    f = pl.program_id(1)

    @pl.when(f == 0)
    def _():
        acc_ref[...] = jnp.zeros_like(acc_ref)

    a = jnp.dot(xn_ref[...], wup_ref[...], preferred_element_type=F32)
    a = jnp.square(jnp.maximum(a, 0.0)).astype(BF16)
    acc_ref[...] += jnp.dot(a, wdn_ref[...], preferred_element_type=F32)

    @pl.when(f == n_f - 1)
    def _():
        h = h_ref[...] + acc_ref[...]
        hout_ref[...] = h
        xn3_ref[...] = _rms(h, g_ref[...]).astype(xn3_ref.dtype)


def _ffn_call(xn, w_up_b, w_down_b, h, g_ple, layer, tm):
    m = h.shape[0]
    n_f = D_FF // TF
    row = lambda i, f: (i, 0)
    return pl.pallas_call(
        functools.partial(_ffn_kernel, n_f=n_f),
        grid=(m // tm, n_f),
        in_specs=[pl.BlockSpec((tm, D_MODEL), row),
                  pl.BlockSpec((None, D_MODEL, TF), lambda i, f: (layer, 0, f)),
                  pl.BlockSpec((None, TF, D_MODEL), lambda i, f: (layer, f, 0)),
                  pl.BlockSpec((tm, D_MODEL), row),
                  pl.BlockSpec((1, D_MODEL), lambda i, f: (0, 0))],
        out_specs=[pl.BlockSpec((tm, D_MODEL), row), pl.BlockSpec((tm, D_MODEL), row)],
        out_shape=(jax.ShapeDtypeStruct((m, D_MODEL), F32),
                   jax.ShapeDtypeStruct((m, D_MODEL), BF16)),
        scratch_shapes=[pltpu.VMEM((tm, D_MODEL), F32)],
        compiler_params=_params(2),
        name="ffn",
    )(xn, w_up_b, w_down_b, h, g_ple)


def _ple_kernel(xn_ref, wg_ref, p_ref, wp_ref, h_ref, g_ref, *out_refs, last):
    gate = jax.nn.sigmoid(jnp.dot(xn_ref[...], wg_ref[...], preferred_element_type=F32))
    proj = jnp.dot(p_ref[...].astype(BF16), wp_ref[...], preferred_element_type=F32)
    h = h_ref[...] + proj * gate
    if last:
        out_refs[0][...] = _rms(h, g_ref[...])
    else:
        out_refs[0][...] = h
        out_refs[1][...] = _rms(h, g_ref[...]).astype(out_refs[1].dtype)


def _ple_call(xn, w_gate_b, p, w_proj_b, h, g_next, layer, tm, last):
    m = h.shape[0]
    row = lambda i: (i, 0)
    hspec = pl.BlockSpec((tm, D_MODEL), row)
    if last:
        out_specs = [hspec]
        out_shape = (jax.ShapeDtypeStruct((m, D_MODEL), F32),)
    else:
        out_specs = [hspec, hspec]
        out_shape = (jax.ShapeDtypeStruct((m, D_MODEL), F32),
                     jax.ShapeDtypeStruct((m, D_MODEL), BF16))
    return pl.pallas_call(
        functools.partial(_ple_kernel, last=last),
        grid=(m // tm,),
        in_specs=[hspec,
                  pl.BlockSpec((None, D_MODEL, D_MODEL), lambda i: (layer, 0, 0)),
                  pl.BlockSpec((None, tm, PLE_DIM), lambda i: (layer, i, 0)),
                  pl.BlockSpec((None, PLE_DIM, D_MODEL), lambda i: (layer, 0, 0)),
                  hspec,
                  pl.BlockSpec((1, D_MODEL), lambda i: (0, 0))],
        out_specs=out_specs,
        out_shape=out_shape,
        compiler_params=_params(1),
        name="ple",
    )(xn, w_gate_b, p, w_proj_b, h, g_next)


def kernel(x_prompt, x_sample, cache_k, cache_v, page_table, p_prompt, p_sample,
           g_mix, w_in, lam_q1, lam_k1, lam_q2, lam_k2, g_sub, g_v, w_s, b_s, w_out,
           g_ffn, w_up, w_down, g_ple, w_ple_gate, w_ple_proj, g_final):
    batch, seq, d = x_prompt.shape
    db = x_sample.shape[0]
    depth = w_in.shape[0]
    mp = batch * seq

    w_in_b = w_in.astype(BF16)
    w_out_b = w_out.astype(BF16)
    w_up_b = w_up.astype(BF16)
    w_down_b = w_down.astype(BF16)
    w_gate_b = w_ple_gate.astype(BF16)
    w_proj_b = w_ple_proj.astype(BF16)
    pp = p_prompt.reshape(depth, mp, PLE_DIM)
    ps = p_sample.reshape(depth, db, PLE_DIM)

    hp = x_prompt.reshape(mp, d)
    hs = x_sample.reshape(db, d)
    xnp = _norm_call(hp, g_mix[0].reshape(1, d), TM_PROMPT)
    xns = _norm_call(hs, g_mix[0].reshape(1, d), db)

    kp, vp, ks, vs, gs = [], [], [], [], []
    for i in range(depth):
        lam_init = 0.8 - 0.6 * math.exp(-0.3 * i)
        last = i == depth - 1
        lam_vecs = jnp.stack([lam_q1[i], lam_k1[i], lam_q2[i], lam_k2[i]])
        gsub_i = g_sub[i].reshape(1, V_DIM)
        gv_i = g_v[i].reshape(1, GMLP_WIDTH)
        gffn_i = g_ffn[i].reshape(1, d)
        gple_i = g_ple[i].reshape(1, d)
        gnext = (g_final if last else g_mix[i + 1]).reshape(1, d)
        bias_full = jnp.repeat(b_s[i].T, GROUP_DIM, axis=1)
        ws00 = jnp.repeat(w_s[i, :, 0, 0], GROUP_DIM).reshape(1, GMLP_WIDTH)
        bias00 = bias_full[0:1]

        qa, qb, k, v, u, gv = _inproj_call(xnp, w_in_b, gv_i, i, TM_PROMPT, BF16)
        attn = _attn_call(lam_vecs, qa, qb, k, v, gsub_i, lam_init, batch, seq)
        hp, xn2 = _mixout_call(attn, u, gv, w_s[i], bias_full, w_out_b, hp, gffn_i, i,
                               TM_PROMPT, False)
        hp, xn3 = _ffn_call(xn2, w_up_b, w_down_b, hp, gple_i, i, TM_PROMPT)
        res = _ple_call(xn3, w_gate_b, pp, w_proj_b, hp, gnext, i, TM_PROMPT, last)
        hp = res[0]
        xnp = None if last else res[1]
        kp.append(k)
        vp.append(v)

        qa, qb, k, v, u, gv = _inproj_call(xns, w_in_b, gv_i, i, db, F32)
        attn = _sattn_call(page_table, lam_vecs, qa, qb, k, v, gsub_i, cache_k, cache_v,
                           i, lam_init)
        hs, xn2 = _mixout_call(attn, u, gv, ws00, bias00, w_out_b, hs, gffn_i, i, db, True)
        hs, xn3 = _ffn_call(xn2, w_up_b, w_down_b, hs, gple_i, i, db)
        res = _ple_call(xn3, w_gate_b, ps, w_proj_b, hs, gnext, i, db, last)
        hs = res[0]
        xns = None if last else res[1]
        ks.append(k)
        vs.append(v)
        gs.append(gv)

    y_prompt = hp.reshape(batch, seq, d)
    y_sample = hs.reshape(db, 1, d)
    return (y_prompt, y_sample,
            jnp.stack(kp).reshape(depth, batch, seq, N_HEADS, QK_DIM),
            jnp.stack(vp).reshape(depth, batch, seq, N_HEADS, V_DIM),
            jnp.stack(ks).reshape(depth, db, 1, N_HEADS, QK_DIM),
            jnp.stack(vs).reshape(depth, db, 1, N_HEADS, V_DIM),
            jnp.stack(gs).reshape(depth, db, 1, GMLP_WIDTH))
```

```python
import functools
import math

import jax
import jax.numpy as jnp
from jax import lax
from jax.experimental import pallas as pl
from jax.experimental.pallas import tpu as pltpu

F32 = jnp.float32
BF16 = jnp.bfloat16

D_MODEL = 2048
N_HEADS = 8
HEAD_DIM = 64
QK_DIM = 128
V_DIM = 128
ATTN_WIDTH = 1024
GMLP_WIDTH = 1024
N_GROUPS = 8
GROUP_DIM = 128
CHUNK = 128
D_FF = 4 * D_MODEL
PLE_DIM = 256
PAGE_SIZE = 128
RMS_EPS = 1e-6
NEG_INF = -1e30

VMEM_LIMIT_BYTES = 56 * 1024 * 1024

TM_PROMPT = 512
TF = 512
TQ = 512
PAGES_PER_STEP = 8


def _params(n_axes):
    return pltpu.CompilerParams(
        dimension_semantics=("arbitrary",) * n_axes,
        vmem_limit_bytes=VMEM_LIMIT_BYTES)


def _rms(x, g):
    ms = jnp.mean(x * x, axis=-1, keepdims=True)
    return x * lax.rsqrt(ms + RMS_EPS) * g


def _dot_nt(a, b):
    return lax.dot_general(a, b, (((1,), (1,)), ((), ())), preferred_element_type=F32)


def _norm_kernel(x_ref, g_ref, o_ref):
    o_ref[...] = _rms(x_ref[...], g_ref[...]).astype(o_ref.dtype)


def _norm_call(x, g, tm):
    m, d = x.shape
    return pl.pallas_call(
        _norm_kernel,
        grid=(m // tm,),
        in_specs=[pl.BlockSpec((tm, d), lambda i: (i, 0)),
                  pl.BlockSpec((1, d), lambda i: (0, 0))],
        out_specs=pl.BlockSpec((tm, d), lambda i: (i, 0)),
        out_shape=jax.ShapeDtypeStruct((m, d), BF16),
        compiler_params=_params(1),
        name="rmsnorm",
    )(x, g)


def _store_heads(ref, z):
    tm = z.shape[0]
    for h in range(N_HEADS):
        ref[pl.ds(h, tm, stride=N_HEADS), :] = z[:, h * QK_DIM:(h + 1) * QK_DIM]


def _inproj_kernel(xn_ref, w_ref, gv_g_ref, *refs):
    qa_ref, qb_ref, k_ref, v_ref, u_ref, gv_ref = refs[-6:]
    grp = pl.program_id(1)
    z = jnp.dot(xn_ref[...], w_ref[...], preferred_element_type=F32)

    @pl.when(grp == 0)
    def _():
        zq = z * (HEAD_DIM ** -0.5)
        lane = lax.broadcasted_iota(jnp.int32, zq.shape, 1)
        first = (lane & (QK_DIM - 1)) < HEAD_DIM
        qa_ref[...] = jnp.where(first, zq, 0.0).astype(qa_ref.dtype)
        qb_ref[...] = jnp.where(first, 0.0, zq).astype(qb_ref.dtype)

    @pl.when(grp == 1)
    def _():
        _store_heads(k_ref, z)

    @pl.when(grp == 2)
    def _():
        _store_heads(v_ref, z)

    @pl.when(grp == 3)
    def _():
        u_ref[...] = jax.nn.gelu(z).astype(u_ref.dtype)

    @pl.when(grp == 4)
    def _():
        gv_ref[...] = _rms(jax.nn.gelu(z), gv_g_ref[...]).astype(gv_ref.dtype)


def _inproj_call(xn, w_in_b, g_v, layer, depth, tm, act_dtype, kv_all=None):
    m = xn.shape[0]
    row = lambda i, g: (i, 0)
    wide = pl.BlockSpec((tm, 1024), row)
    heads = pl.BlockSpec((None, tm * N_HEADS, QK_DIM), lambda i, g: (layer, i, 0))
    out_shape = (
        jax.ShapeDtypeStruct((m, ATTN_WIDTH), BF16),
        jax.ShapeDtypeStruct((m, ATTN_WIDTH), BF16),
        jax.ShapeDtypeStruct((depth, m * N_HEADS, QK_DIM), F32),
        jax.ShapeDtypeStruct((depth, m * N_HEADS, V_DIM), F32),
        jax.ShapeDtypeStruct((m, GMLP_WIDTH), act_dtype),
        jax.ShapeDtypeStruct((m, GMLP_WIDTH), act_dtype),
    )
    in_specs = [pl.BlockSpec((tm, D_MODEL), row),
                pl.BlockSpec((None, D_MODEL, 1024), lambda i, g: (layer, 0, g)),
                pl.BlockSpec((1, GMLP_WIDTH), lambda i, g: (0, 0))]
    args = [xn, w_in_b, g_v]
    aliases = {}
    if kv_all is not None:
        in_specs += [pl.BlockSpec(memory_space=pl.ANY)] * 2
        args += list(kv_all)
        aliases = {3: 2, 4: 3}
    return pl.pallas_call(
        _inproj_kernel,
        grid=(m // tm, 5),
        in_specs=in_specs,
        out_specs=[wide, wide, heads, heads, wide, wide],
        out_shape=out_shape,
        input_output_aliases=aliases,
        compiler_params=_params(2),
        name="inproj",
    )(*args)


def _lam(lam_ref, lam_init):
    lv = lam_ref[...]
    a = jnp.sum(lv[0:1] * lv[1:2], axis=-1, keepdims=True)
    b = jnp.sum(lv[2:3] * lv[3:4], axis=-1, keepdims=True)
    return jnp.exp(a) - jnp.exp(b) + lam_init


def _neg_slope(head_plus_one):
    bits = (127 - head_plus_one) << 23
    return -lax.bitcast_convert_type(bits, F32)


def _attn_kernel(lam_ref, qa_ref, qb_ref, k_ref, v_ref, gsub_ref, o_ref, kb_ref, vb_ref,
                 *, lam_init, seq):
    head = pl.program_id(1)
    n_tiles = seq // TQ
    rows = 2 * TQ

    pos = lax.broadcasted_iota(jnp.int32, (seq, QK_DIM), 0)
    lane = lax.broadcasted_iota(jnp.int32, (seq, QK_DIM), 1)
    kext = jnp.where(lane == 0, (pos >> 7).astype(F32),
                     jnp.where(lane == 1, (pos & 127).astype(F32),
                               jnp.where(lane < 4, 1.0, 0.0)))
    kb_ref[:, :QK_DIM] = k_ref[pl.ds(head, seq, stride=N_HEADS), :].astype(BF16)
    kb_ref[:, QK_DIM:] = kext.astype(BF16)
    vb_ref[:, :V_DIM] = v_ref[pl.ds(head, seq, stride=N_HEADS), :].astype(BF16)
    vb_ref[:, V_DIM:] = jnp.ones((seq, V_DIM), BF16)

    slope = -_neg_slope(jnp.full((1, 1), head + 1, jnp.int32))
    qrow = lax.broadcasted_iota(jnp.int32, (rows, QK_DIM), 0) & (TQ - 1)
    qlane = lax.broadcasted_iota(jnp.int32, (rows, QK_DIM), 1)
    r = lax.broadcasted_iota(jnp.int32, (rows, TQ), 0) & (TQ - 1)
    c = lax.broadcasted_iota(jnp.int32, (rows, TQ), 1)
    causal = c <= r
    lam = _lam(lam_ref, lam_init)
    gsub = gsub_ref[...]

    for qi in range(n_tiles):
        qs = slice(qi * TQ, (qi + 1) * TQ)
        qpos = qi * TQ + qrow
        qext = jnp.where(qlane == 0, slope * 128.0,
                         jnp.where(qlane == 1, slope,
                                   jnp.where(qlane == 2,
                                             -(slope * 128.0) * (qpos >> 7).astype(F32),
                                             jnp.where(qlane == 3,
                                                       -slope * (qpos & 127).astype(F32),
                                                       0.0))))
        qq = jnp.concatenate(
            [jnp.concatenate([qa_ref[qs, :], qb_ref[qs, :]], axis=0), qext.astype(BF16)],
            axis=1)
        m = jnp.full((rows, 1), NEG_INF, F32)
        acc = jnp.zeros((rows, 2 * V_DIM), F32)
        for j in range(qi + 1):
            ks = slice(j * TQ, (j + 1) * TQ)
            s = _dot_nt(qq, kb_ref[ks, :])
            if j == qi:
                s = jnp.where(causal, s, NEG_INF)
            m_new = jnp.maximum(m, jnp.max(s, axis=-1, keepdims=True))
            alpha = jnp.exp(m - m_new)
            p = jnp.exp(s - m_new)
            acc = alpha * acc + jnp.dot(p.astype(BF16), vb_ref[ks, :],
                                        preferred_element_type=F32)
            m = m_new
        o = acc[:, :V_DIM] / acc[:, V_DIM:]
        o = o[:TQ] - lam * o[TQ:]
        o_ref[qs, :] = (_rms(o, gsub) * (1.0 - lam_init)).astype(o_ref.dtype)


def _attn_call(lam_vecs, qa, qb, k, v, g_sub, layer, lam_init, batch, seq):
    qspec = pl.BlockSpec((seq, QK_DIM), lambda b, h: (b, h))
    kvspec = pl.BlockSpec((None, seq * N_HEADS, QK_DIM), lambda b, h: (layer, b, 0),
                          pipeline_mode=pl.Buffered(1))
    return pl.pallas_call(
        functools.partial(_attn_kernel, lam_init=lam_init, seq=seq),
        grid=(batch, N_HEADS),
        in_specs=[pl.BlockSpec((4, HEAD_DIM), lambda b, h: (0, 0)),
                  qspec, qspec, kvspec, kvspec,
                  pl.BlockSpec((1, V_DIM), lambda b, h: (0, 0))],
        out_specs=pl.BlockSpec((seq, V_DIM), lambda b, h: (b, h)),
        out_shape=jax.ShapeDtypeStruct((batch * seq, ATTN_WIDTH), BF16),
        scratch_shapes=[pltpu.VMEM((seq, 2 * QK_DIM), BF16),
                        pltpu.VMEM((seq, 2 * V_DIM), BF16)],
        compiler_params=_params(2),
        name="prompt_attn",
    )(lam_vecs, qa, qb, k, v, g_sub)


def _sattn_kernel(pt_ref, lam_ref, qa_ref, qb_ref, kn_ref, vn_ref, gsub_ref, *rest,
                  lam_init, past, n_steps):
    del pt_ref
    npg = PAGES_PER_STEP
    k_refs = rest[:npg]
    v_refs = rest[npg:2 * npg]
    o_ref, m_sc, l_sc, acc_sc = rest[2 * npg:]
    j = pl.program_id(1)
    nrow = 2 * N_HEADS
    page_cols = PAGE_SIZE * N_HEADS
    span = npg * page_cols

    qrows = jnp.concatenate([qa_ref[...], qb_ref[...]], axis=0)

    def own(width):
        row = lax.broadcasted_iota(jnp.int32, (nrow, width), 0)
        col = lax.broadcasted_iota(jnp.int32, (nrow, width), 1)
        return (col & (N_HEADS - 1)) == (row & (N_HEADS - 1))

    @pl.when(j == 0)
    def _():
        kn = kn_ref[...]
        kn_rep = jnp.concatenate([kn] * (QK_DIM // N_HEADS), axis=0).astype(BF16)
        s0 = jnp.where(own(QK_DIM), _dot_nt(qrows, kn_rep), NEG_INF)
        m_sc[...] = jnp.max(s0, axis=-1, keepdims=True)
        l_sc[...] = jnp.ones_like(l_sc)
        vn = vn_ref[...]
        acc_sc[...] = jnp.concatenate([vn, vn], axis=0)

    kcat = jnp.concatenate([r[...].reshape(page_cols, QK_DIM) for r in k_refs],
                           axis=0).astype(BF16)
    vcat = jnp.concatenate([r[...].reshape(page_cols, V_DIM) for r in v_refs],
                           axis=0).astype(BF16)
    s = _dot_nt(qrows, kcat)
    col = lax.broadcasted_iota(jnp.int32, (1, span), 1)
    kpos = j * (npg * PAGE_SIZE) + (col >> 3)
    dist = (past - kpos).astype(F32)
    head1 = (lax.broadcasted_iota(jnp.int32, (nrow, 1), 0) & (N_HEADS - 1)) + 1
    s = jnp.where(own(span), s + _neg_slope(head1) * dist, NEG_INF)
    m = m_sc[...]
    m_new = jnp.maximum(m, jnp.max(s, axis=-1, keepdims=True))
    alpha = jnp.exp(m - m_new)
    p = jnp.exp(s - m_new)
    l_sc[...] = alpha * l_sc[...] + jnp.sum(p, axis=-1, keepdims=True)
    acc_sc[...] = alpha * acc_sc[...] + jnp.dot(p.astype(BF16), vcat,
                                                preferred_element_type=F32)
    m_sc[...] = m_new

    @pl.when(j == n_steps - 1)
    def _():
        o = acc_sc[...] / l_sc[...]
        lam = _lam(lam_ref, lam_init)
        o = o[:N_HEADS] - lam * o[N_HEADS:]
        o_ref[...] = (_rms(o, gsub_ref[...]) * (1.0 - lam_init)).astype(o_ref.dtype)


def _sattn_call(page_table, lam_vecs, qa, qb, k_new, v_new, g_sub, cache_k, cache_v,
                layer, lam_init):
    db, n_pages = page_table.shape
    npg = PAGES_PER_STEP
    n_steps = n_pages // npg
    past = n_pages * PAGE_SIZE
    q3spec = pl.BlockSpec((None, N_HEADS, QK_DIM), lambda b, j, pt: (b, 0, 0))
    newspec = pl.BlockSpec((None, N_HEADS, QK_DIM), lambda b, j, pt: (layer, b, 0))

    def page_spec(p):
        return pl.BlockSpec((None, None, PAGE_SIZE, N_HEADS, QK_DIM),
                            lambda b, j, pt: (layer, pt[b, j * npg + p], 0, 0, 0))

    grid_spec = pltpu.PrefetchScalarGridSpec(
        num_scalar_prefetch=1,
        grid=(db, n_steps),
        in_specs=[pl.BlockSpec((4, HEAD_DIM), lambda b, j, pt: (0, 0)),
                  q3spec, q3spec, newspec, newspec,
                  pl.BlockSpec((1, V_DIM), lambda b, j, pt: (0, 0))]
                 + [page_spec(p) for p in range(npg)] * 2,
        out_specs=q3spec,
        scratch_shapes=[pltpu.VMEM((2 * N_HEADS, 1), F32), pltpu.VMEM((2 * N_HEADS, 1), F32),
                        pltpu.VMEM((2 * N_HEADS, V_DIM), F32)],
    )
    q3 = lambda a: a.reshape(db, N_HEADS, QK_DIM)
    out = pl.pallas_call(
        functools.partial(_sattn_kernel, lam_init=lam_init, past=past, n_steps=n_steps),
        grid_spec=grid_spec,
        out_shape=jax.ShapeDtypeStruct((db, N_HEADS, V_DIM), F32),
        compiler_params=_params(2),
        name="sample_attn",
    )(page_table, lam_vecs, q3(qa), q3(qb), k_new, v_new, g_sub,
      *([cache_k] * npg), *([cache_v] * npg))
    return out.reshape(db, N_HEADS * V_DIM)


def _mixout_kernel(attn_ref, u_ref, gv_ref, ws_ref, bias_ref, w_ref, h_ref, g_ref,
                   hout_ref, xn_ref, *, single_position):
    u = u_ref[...].astype(F32)
    if single_position:
        mixed = gv_ref[...].astype(F32) * ws_ref[...] + bias_ref[...]
        sg = u * mixed
    else:
        tm = u.shape[0]
        n_chunks = tm // CHUNK
        tr = lax.broadcasted_iota(jnp.int32, (CHUNK, CHUNK), 0)
        tc = lax.broadcasted_iota(jnp.int32, (CHUNK, CHUNK), 1)
        gv = gv_ref[...]
        cols = []
        for g in range(N_GROUPS):
            gs = slice(g * GROUP_DIM, (g + 1) * GROUP_DIM)
            w = jnp.where(tc <= tr, ws_ref[g], 0.0).astype(BF16)
            rhs = jnp.concatenate(
                [gv[n * CHUNK:(n + 1) * CHUNK, gs] for n in range(n_chunks)], axis=1)
            mixed = jnp.dot(w, rhs, preferred_element_type=F32)
            b = bias_ref[:, gs]
            cols.append(jnp.concatenate(
                [mixed[:, n * GROUP_DIM:(n + 1) * GROUP_DIM] + b for n in range(n_chunks)],
                axis=0))
        sg = u * jnp.concatenate(cols, axis=1)
    hw = ATTN_WIDTH
    y = (jnp.dot(attn_ref[...].astype(BF16), w_ref[:hw, :], preferred_element_type=F32)
         + jnp.dot(sg.astype(BF16), w_ref[hw:, :], preferred_element_type=F32))
    h = h_ref[...] + y
    hout_ref[...] = h
    xn_ref[...] = _rms(h, g_ref[...]).astype(xn_ref.dtype)


def _mixout_call(attn, u, gv, ws, bias, w_out_b, h, g_ffn, layer, tm, single_position):
    m = h.shape[0]
    row = lambda i: (i, 0)
    const2 = lambda i: (0, 0)
    if single_position:
        ws_spec = pl.BlockSpec((1, GMLP_WIDTH), const2)
        bias_spec = pl.BlockSpec((1, GMLP_WIDTH), const2)
    else:
        ws_spec = pl.BlockSpec((N_GROUPS, CHUNK, CHUNK), lambda i: (0, 0, 0))
        bias_spec = pl.BlockSpec((CHUNK, GMLP_WIDTH), const2)
    return pl.pallas_call(
        functools.partial(_mixout_kernel, single_position=single_position),
        grid=(m // tm,),
        in_specs=[pl.BlockSpec((tm, ATTN_WIDTH), row),
                  pl.BlockSpec((tm, GMLP_WIDTH), row),
                  pl.BlockSpec((tm, GMLP_WIDTH), row),
                  ws_spec, bias_spec,
                  pl.BlockSpec((None, D_MODEL, D_MODEL), lambda i: (layer, 0, 0)),
                  pl.BlockSpec((tm, D_MODEL), row),
                  pl.BlockSpec((1, D_MODEL), const2)],
        out_specs=[pl.BlockSpec((tm, D_MODEL), row), pl.BlockSpec((tm, D_MODEL), row)],
        out_shape=(jax.ShapeDtypeStruct((m, D_MODEL), F32),
                   jax.ShapeDtypeStruct((m, D_MODEL), BF16)),
        compiler_params=_params(1),
        name="mix_out",
    )(attn, u, gv, ws, bias, w_out_b, h, g_ffn)


def _ffn_kernel(xn_ref, wup_ref, wdn_ref, h_ref, g_ref, hout_ref, xn3_ref, acc_ref, *, n_f):
    f = pl.program_id(1)

    @pl.when(f == 0)
    def _():
        acc_ref[...] = jnp.zeros_like(acc_ref)

    a = jnp.dot(xn_ref[...], wup_ref[...], preferred_element_type=F32)
    a = jnp.square(jnp.maximum(a, 0.0)).astype(BF16)
    acc_ref[...] += jnp.dot(a, wdn_ref[...], preferred_element_type=F32)

    @pl.when(f == n_f - 1)
    def _():
        h = h_ref[...] + acc_ref[...]
        hout_ref[...] = h
        xn3_ref[...] = _rms(h, g_ref[...]).astype(xn3_ref.dtype)


def _ffn_call(xn, w_up_b, w_down_b, h, g_ple, layer, tm):
    m = h.shape[0]
    n_f = D_FF // TF
    row = lambda i, f: (i, 0)
    return pl.pallas_call(
        functools.partial(_ffn_kernel, n_f=n_f),
        grid=(m // tm, n_f),
        in_specs=[pl.BlockSpec((tm, D_MODEL), row),
                  pl.BlockSpec((None, D_MODEL, TF), lambda i, f: (layer, 0, f)),
                  pl.BlockSpec((None, TF, D_MODEL), lambda i, f: (layer, f, 0)),
                  pl.BlockSpec((tm, D_MODEL), row),
                  pl.BlockSpec((1, D_MODEL), lambda i, f: (0, 0))],
        out_specs=[pl.BlockSpec((tm, D_MODEL), row), pl.BlockSpec((tm, D_MODEL), row)],
        out_shape=(jax.ShapeDtypeStruct((m, D_MODEL), F32),
                   jax.ShapeDtypeStruct((m, D_MODEL), BF16)),
        scratch_shapes=[pltpu.VMEM((tm, D_MODEL), F32)],
        compiler_params=_params(2),
        name="ffn",
    )(xn, w_up_b, w_down_b, h, g_ple)


def _ple_kernel(xn_ref, wg_ref, p_ref, wp_ref, h_ref, g_ref, *out_refs, last):
    gate = jax.nn.sigmoid(jnp.dot(xn_ref[...], wg_ref[...], preferred_element_type=F32))
    proj = jnp.dot(p_ref[...].astype(BF16), wp_ref[...], preferred_element_type=F32)
    h = h_ref[...] + proj * gate
    if last:
        out_refs[0][...] = _rms(h, g_ref[...])
    else:
        out_refs[0][...] = h
        out_refs[1][...] = _rms(h, g_ref[...]).astype(out_refs[1].dtype)


def _ple_call(xn, w_gate_b, p, w_proj_b, h, g_next, layer, tm, last):
    m = h.shape[0]
    row = lambda i: (i, 0)
    hspec = pl.BlockSpec((tm, D_MODEL), row)
    if last:
        out_specs = [hspec]
        out_shape = (jax.ShapeDtypeStruct((m, D_MODEL), F32),)
    else:
        out_specs = [hspec, hspec]
        out_shape = (jax.ShapeDtypeStruct((m, D_MODEL), F32),
                     jax.ShapeDtypeStruct((m, D_MODEL), BF16))
    return pl.pallas_call(
        functools.partial(_ple_kernel, last=last),
        grid=(m // tm,),
        in_specs=[hspec,
                  pl.BlockSpec((None, D_MODEL, D_MODEL), lambda i: (layer, 0, 0)),
                  pl.BlockSpec((None, tm, PLE_DIM), lambda i: (layer, i, 0)),
                  pl.BlockSpec((None, PLE_DIM, D_MODEL), lambda i: (layer, 0, 0)),
                  hspec,
                  pl.BlockSpec((1, D_MODEL), lambda i: (0, 0))],
        out_specs=out_specs,
        out_shape=out_shape,
        compiler_params=_params(1),
        name="ple",
    )(xn, w_gate_b, p, w_proj_b, h, g_next)


def kernel(x_prompt, x_sample, cache_k, cache_v, page_table, p_prompt, p_sample,
           g_mix, w_in, lam_q1, lam_k1, lam_q2, lam_k2, g_sub, g_v, w_s, b_s, w_out,
           g_ffn, w_up, w_down, g_ple, w_ple_gate, w_ple_proj, g_final):
    batch, seq, d = x_prompt.shape
    db = x_sample.shape[0]
    depth = w_in.shape[0]
    mp = batch * seq

    w_in_b = w_in.astype(BF16)
    w_out_b = w_out.astype(BF16)
    w_up_b = w_up.astype(BF16)
    w_down_b = w_down.astype(BF16)
    w_gate_b = w_ple_gate.astype(BF16)
    w_proj_b = w_ple_proj.astype(BF16)
    pp = p_prompt.reshape(depth, mp, PLE_DIM)
    ps = p_sample.reshape(depth, db, PLE_DIM)

    hp = x_prompt.reshape(mp, d)
    hs = x_sample.reshape(db, d)
    xnp = _norm_call(hp, g_mix[0].reshape(1, d), TM_PROMPT)
    xns = _norm_call(hs, g_mix[0].reshape(1, d), db)

    kvp = kvs = None
    gs = []
    for i in range(depth):
        lam_init = 0.8 - 0.6 * math.exp(-0.3 * i)
        last = i == depth - 1
        lam_vecs = jnp.stack([lam_q1[i], lam_k1[i], lam_q2[i], lam_k2[i]])
        gsub_i = g_sub[i].reshape(1, V_DIM)
        gv_i = g_v[i].reshape(1, GMLP_WIDTH)
        gffn_i = g_ffn[i].reshape(1, d)
        gple_i = g_ple[i].reshape(1, d)
        gnext = (g_final if last else g_mix[i + 1]).reshape(1, d)
        bias_full = jnp.repeat(b_s[i].T, GROUP_DIM, axis=1)
        ws00 = jnp.repeat(w_s[i, :, 0, 0], GROUP_DIM).reshape(1, GMLP_WIDTH)
        bias00 = bias_full[0:1]

        qa, qb, k, v, u, gv = _inproj_call(xnp, w_in_b, gv_i, i, depth, TM_PROMPT, BF16, kvp)
        kvp = (k, v)
        attn = _attn_call(lam_vecs, qa, qb, k, v, gsub_i, i, lam_init, batch, seq)
        hp, xn2 = _mixout_call(attn, u, gv, w_s[i], bias_full, w_out_b, hp, gffn_i, i,
                               TM_PROMPT, False)
        hp, xn3 = _ffn_call(xn2, w_up_b, w_down_b, hp, gple_i, i, TM_PROMPT)
        res = _ple_call(xn3, w_gate_b, pp, w_proj_b, hp, gnext, i, TM_PROMPT, last)
        hp = res[0]
        xnp = None if last else res[1]

        qa, qb, k, v, u, gv = _inproj_call(xns, w_in_b, gv_i, i, depth, db, F32, kvs)
        kvs = (k, v)
        attn = _sattn_call(page_table, lam_vecs, qa, qb, k, v, gsub_i, cache_k, cache_v,
                           i, lam_init)
        hs, xn2 = _mixout_call(attn, u, gv, ws00, bias00, w_out_b, hs, gffn_i, i, db, True)
        hs, xn3 = _ffn_call(xn2, w_up_b, w_down_b, hs, gple_i, i, db)
        res = _ple_call(xn3, w_gate_b, ps, w_proj_b, hs, gnext, i, db, last)
        hs = res[0]
        xns = None if last else res[1]
        gs.append(gv)

    y_prompt = hp.reshape(batch, seq, d)
    y_sample = hs.reshape(db, 1, d)
    return (y_prompt, y_sample,
            kvp[0].reshape(depth, batch, seq, N_HEADS, QK_DIM),
            kvp[1].reshape(depth, batch, seq, N_HEADS, V_DIM),
            kvs[0].reshape(depth, db, 1, N_HEADS, QK_DIM),
            kvs[1].reshape(depth, db, 1, N_HEADS, V_DIM),
            jnp.stack(gs).reshape(depth, db, 1, GMLP_WIDTH))
```
